```python
import jax, jax.numpy as jnp
from jax import lax
import numpy as np

D_MODEL = 2048
BATCH = 16
SEQ = 2048
DEPTH = 1
DEC_BATCH = 4
DEC_SEQ = 8192
PAST_LEN = 128

HEAD_DIM = 64
A_HEADS = 16
A_KV_HEADS = 4
A_WINDOW = 128
A_BLOCK = 128
B_HEADS = 16
B_PATTERNS = ((128, 1), (512, 4), (2048, 16))
B_BLOCK = 64
ROPE_THETA = 10000.0
A_Q = A_HEADS * HEAD_DIM
A_KV = A_KV_HEADS * HEAD_DIM
B_W = B_HEADS * HEAD_DIM
MIX_W = A_Q + B_W
IN_COLS = A_Q + 2 * A_KV + 3 * B_W
MEM_LEN = 256
X_HEADS = 4
X_HEAD_DIM = 128
X_W = X_HEADS * X_HEAD_DIM
PEER_HEADS = 8
PEER_NKEYS = 128
PEER_EXPERTS = PEER_NKEYS * PEER_NKEYS
PEER_DKEY = 256
PEER_TOPK = 16
PEER_CHUNK = 128
EPS = 1e-6
NEG = -1e30

kernel_name = "hymba_style_dilated_peer_encoder"


def rmsnorm(x, g):
    xf = x.astype(jnp.float32)
    y = xf * lax.rsqrt(jnp.mean(xf * xf, axis=-1, keepdims=True) + EPS)
    return (y * g.astype(jnp.float32)).astype(x.dtype)


def rope(x, pos):
    dh = x.shape[-1]
    freqs = jnp.power(ROPE_THETA, -jnp.arange(0, dh, 2, dtype=jnp.float32) / dh)
    ang = pos[:, None] * freqs[None, :]
    cos = jnp.cos(ang)[None, :, None, :]
    sin = jnp.sin(ang)[None, :, None, :]
    xf = x.astype(jnp.float32)
    x1, x2 = xf[..., : dh // 2], xf[..., dh // 2:]
    return jnp.concatenate([x1 * cos - x2 * sin, x1 * sin + x2 * cos], axis=-1).astype(x.dtype)


def banded_attention(q, k, v, half_w, block, sink=None):
    b, L, h, dh = q.shape
    hk = k.shape[2]
    g = h // hk
    nb = -(-L // block)
    Lp = nb * block
    span = block + 2 * half_w
    qb = jnp.pad(q, ((0, 0), (0, Lp - L), (0, 0), (0, 0))).reshape(b, nb, block, hk, g, dh)
    pad = ((0, 0), (half_w, Lp - L + half_w), (0, 0), (0, 0))
    kp = jnp.pad(k, pad)
    vp = jnp.pad(v, pad)
    idx = jnp.arange(nb)[:, None] * block + jnp.arange(span)[None, :]
    kb = kp[:, idx]
    vb = vp[:, idx]
    s = jnp.einsum('bnqkgd,bnskd->bnkgqs', qb, kb, preferred_element_type=jnp.float32) * (dh ** -0.5)
    qpos = jnp.arange(nb)[:, None] * block + jnp.arange(block)[None, :]
    kpos = jnp.arange(nb)[:, None] * block + jnp.arange(span)[None, :] - half_w
    valid = ((kpos[:, None, :] >= 0) & (kpos[:, None, :] < L)
             & (jnp.abs(kpos[:, None, :] - qpos[:, :, None]) <= half_w))
    s = jnp.where(valid[None, :, None, None], s, NEG)
    m = jnp.max(s, axis=-1)
    if sink is not None:
        sink_b = sink.astype(jnp.float32).reshape(1, 1, hk, g, 1)
        m = jnp.maximum(m, sink_b)
    p = jnp.exp(s - m[..., None])
    denom = jnp.sum(p, axis=-1)
    if sink is not None:
        denom = denom + jnp.exp(sink_b - m)
    o = jnp.einsum('bnkgqs,bnskd->bnqkgd', p.astype(v.dtype), vb, preferred_element_type=jnp.float32)
    o = o / jnp.transpose(denom, (0, 1, 4, 2, 3))[..., None]
    o = o.reshape(b, Lp, h, dh)[:, :L].astype(q.dtype)
    lse = jnp.transpose(m + jnp.log(denom), (0, 1, 4, 2, 3)).reshape(b, Lp, h)[:, :L]
    return o, lse


def dilated_mixture(q, k, v):
    b, s, h, dh = q.shape
    outs, lses = [], []
    for w, d in B_PATTERNS:
        half = (w // 2) // d
        def fold(t):
            return t.reshape(b, s // d, d, h, dh).transpose(0, 2, 1, 3, 4).reshape(b * d, s // d, h, dh)
        o, l = banded_attention(fold(q), fold(k), fold(v), half, B_BLOCK)
        outs.append(o.reshape(b, d, s // d, h, dh).transpose(0, 2, 1, 3, 4).reshape(b, s, h, dh))
        lses.append(l.reshape(b, d, s // d, h).transpose(0, 2, 1, 3).reshape(b, s, h))
    alpha = jax.nn.softmax(jnp.stack(lses, axis=0), axis=0)
    out = alpha[0][..., None] * outs[0].astype(jnp.float32)
    for i in range(1, len(outs)):
        out = out + alpha[i][..., None] * outs[i].astype(jnp.float32)
    return out.astype(q.dtype)


def peer(h, w_pq, sub_keys, u_tab, v_tab):
    b, s, d = h.shape
    chunks = h.reshape(-1, PEER_CHUNK, d)

    def one_chunk(hc):
        c = hc.shape[0]
        q = (hc @ w_pq).reshape(c, PEER_HEADS, 2, PEER_DKEY // 2)
        sc = jnp.einsum('chpk,hpnk->chpn', q, sub_keys, preferred_element_type=jnp.float32)
        s1, i1 = lax.top_k(sc[:, :, 0], PEER_TOPK)
        s2, i2 = lax.top_k(sc[:, :, 1], PEER_TOPK)
        cand = (s1[..., :, None] + s2[..., None, :]).reshape(c, PEER_HEADS, PEER_TOPK * PEER_TOPK)
        cidx = (i1[..., :, None] * PEER_NKEYS + i2[..., None, :]).reshape(c, PEER_HEADS, PEER_TOPK * PEER_TOPK)
        top_s, pos = lax.top_k(cand, PEER_TOPK)
        eid = jnp.take_along_axis(cidx, pos, axis=-1)
        gate = jax.nn.softmax(top_s, axis=-1)
        u = u_tab[eid]
        act = jax.nn.gelu(jnp.einsum('chkd,cd->chk', u, hc, preferred_element_type=jnp.float32), approximate=False)
        vv = v_tab[eid]
        return jnp.einsum('chk,chkd->cd', (gate * act).astype(hc.dtype), vv)

    out = lax.map(one_chunk, chunks)
    return out.reshape(b, s, d)


def trunk(x, mem, g_mix, w_in, a_sink, g_a_out, g_b_out, w_out, g_cross, g_mem,
          w_xq, w_xkv, w_xo, g_ffn, w_pq, peer_keys, peer_u, peer_v, g_final):
    b, s, _ = x.shape
    m_len = mem.shape[1]
    pos = jnp.arange(s, dtype=jnp.float32)
    splits = [A_Q, A_Q + A_KV, A_Q + 2 * A_KV, A_Q + 2 * A_KV + B_W, A_Q + 2 * A_KV + 2 * B_W]
    for l in range(DEPTH):
        hn = rmsnorm(x, g_mix[l])
        proj = hn @ w_in[l]
        aq, ak, av, bq, bk, bv = jnp.split(proj, splits, axis=-1)
        aq = rope(aq.reshape(b, s, A_HEADS, HEAD_DIM), pos)
        ak = rope(ak.reshape(b, s, A_KV_HEADS, HEAD_DIM), pos)
        av = av.reshape(b, s, A_KV_HEADS, HEAD_DIM)
        a_out, _ = banded_attention(aq, ak, av, A_WINDOW, A_BLOCK, sink=a_sink[l])
        bq = rope(bq.reshape(b, s, B_HEADS, HEAD_DIM), pos)
        bk = rope(bk.reshape(b, s, B_HEADS, HEAD_DIM), pos)
        bv = bv.reshape(b, s, B_HEADS, HEAD_DIM)
        b_out = dilated_mixture(bq, bk, bv)
        mix = jnp.concatenate([rmsnorm(a_out.reshape(b, s, A_Q), g_a_out[l]),
                               rmsnorm(b_out.reshape(b, s, B_W), g_b_out[l])], axis=-1)
        x = x + mix @ w_out[l]
        xq = (rmsnorm(x, g_cross[l]) @ w_xq[l]).reshape(b, s, X_HEADS, X_HEAD_DIM)
        kv = rmsnorm(mem, g_mem[l]) @ w_xkv[l]
        mk = kv[..., :X_W].reshape(b, m_len, X_HEADS, X_HEAD_DIM)
        mv = kv[..., X_W:].reshape(b, m_len, X_HEADS, X_HEAD_DIM)
        sc = jnp.einsum('bshd,bmhd->bhsm', xq, mk, preferred_element_type=jnp.float32) * (X_HEAD_DIM ** -0.5)
        p = jax.nn.softmax(sc, axis=-1)
        xo = jnp.einsum('bhsm,bmhd->bshd', p.astype(mv.dtype), mv).reshape(b, s, X_W)
        x = x + xo @ w_xo[l]
        x = x + peer(rmsnorm(x, g_ffn[l]), w_pq[l], peer_keys[l], peer_u[l], peer_v[l])
    return rmsnorm(x, g_final)


def setup_inputs(seed: int = 0) -> dict:
    key = jax.random.key(seed)
    ks = jax.random.split(key, 24)
    f32 = jnp.float32
    nrm = lambda k, shape, scale: (jax.random.normal(k, shape, f32) * scale)
    gain = lambda k, shape: 1.0 + 0.02 * jax.random.normal(k, shape, f32)
    L = DEPTH
    return {
        "x_prompt": nrm(ks[0], (BATCH, SEQ, D_MODEL), 1.0),
        "x_sample": nrm(ks[1], (DEC_BATCH, DEC_SEQ, D_MODEL), 1.0),
        "mem_prompt": nrm(ks[2], (BATCH, MEM_LEN, D_MODEL), 1.0),
        "mem_sample": nrm(ks[3], (DEC_BATCH, MEM_LEN, D_MODEL), 1.0),
        "g_mix": gain(ks[4], (L, D_MODEL)),
        "w_in": nrm(ks[5], (L, D_MODEL, IN_COLS), D_MODEL ** -0.5),
        "a_sink": nrm(ks[6], (L, A_HEADS), 0.5),
        "g_a_out": gain(ks[7], (L, A_Q)),
        "g_b_out": gain(ks[8], (L, B_W)),
        "w_out": nrm(ks[9], (L, MIX_W, D_MODEL), MIX_W ** -0.5),
        "g_cross": gain(ks[10], (L, D_MODEL)),
        "g_mem": gain(ks[11], (L, D_MODEL)),
        "w_xq": nrm(ks[12], (L, D_MODEL, X_W), D_MODEL ** -0.5),
        "w_xkv": nrm(ks[13], (L, D_MODEL, 2 * X_W), D_MODEL ** -0.5),
        "w_xo": nrm(ks[14], (L, X_W, D_MODEL), X_W ** -0.5),
        "g_ffn": gain(ks[15], (L, D_MODEL)),
        "w_pq": nrm(ks[16], (L, D_MODEL, PEER_HEADS * PEER_DKEY), D_MODEL ** -0.5),
        "peer_keys": nrm(ks[17], (L, PEER_HEADS, 2, PEER_NKEYS, PEER_DKEY // 2), (PEER_DKEY // 2) ** -0.5),
        "peer_u": nrm(ks[18], (L, PEER_EXPERTS, D_MODEL), D_MODEL ** -0.5),
        "peer_v": nrm(ks[19], (L, PEER_EXPERTS, D_MODEL), 0.5),
        "g_final": gain(ks[20], (D_MODEL,)),
    }


def reference(x_prompt, x_sample, mem_prompt, mem_sample, g_mix, w_in, a_sink, g_a_out, g_b_out,
              w_out, g_cross, g_mem, w_xq, w_xkv, w_xo, g_ffn, w_pq, peer_keys, peer_u, peer_v, g_final):
    y_prompt = trunk(x_prompt, mem_prompt, g_mix, w_in, a_sink, g_a_out, g_b_out, w_out, g_cross, g_mem,
                     w_xq, w_xkv, w_xo, g_ffn, w_pq, peer_keys, peer_u, peer_v, g_final)
    y_sample = trunk(x_sample, mem_sample, g_mix, w_in, a_sink, g_a_out, g_b_out, w_out, g_cross, g_mem,
                     w_xq, w_xkv, w_xo, g_ffn, w_pq, peer_keys, peer_u, peer_v, g_final)
    return (y_prompt, y_sample)
```

```python
import functools

import jax
import jax.numpy as jnp
from jax import lax
from jax.experimental import pallas as pl
from jax.experimental.pallas import tpu as pltpu

D_MODEL = 2048
HEAD_DIM = 64
A_HEADS = 16
A_KV_HEADS = 4
A_WINDOW = 128
B_HEADS = 16
B_PATTERNS = ((128, 1), (512, 4), (2048, 16))
ROPE_THETA = 10000.0
A_Q = A_HEADS * HEAD_DIM
A_KV = A_KV_HEADS * HEAD_DIM
B_W = B_HEADS * HEAD_DIM
X_HEADS = 4
X_HEAD_DIM = 128
X_W = X_HEADS * X_HEAD_DIM
PEER_HEADS = 8
PEER_NKEYS = 128
PEER_TOPK = 16
PEER_PAIRS = PEER_HEADS * PEER_TOPK
EPS = 1e-6
NEG = -1e30

LANES = 128
VMEM_LIMIT = 56 * 1024 * 1024

ROW_TILE = 256
ATTN_TILE = 128
ROPE_CHUNK = 512
EXPERT_TILE = 16
HALF_D = D_MODEL // 2

_NT = (((1,), (1,)), ((), ()))


def _params(*sem):
    return pltpu.CompilerParams(dimension_semantics=sem, vmem_limit_bytes=VMEM_LIMIT)


def _rms(xf, g):
    return xf * lax.rsqrt(jnp.mean(xf * xf, axis=-1, keepdims=True) + EPS) * g


def _rope(acc, cos, sin):
    n = acc.shape[-1]
    lane = lax.broadcasted_iota(jnp.int32, acc.shape, 1)
    first = (lane % HEAD_DIM) < (HEAD_DIM // 2)
    partner = jnp.where(first, pltpu.roll(acc, n - HEAD_DIM // 2, 1), pltpu.roll(acc, HEAD_DIM // 2, 1))
    return acc * cos + partner * sin


def _inproj_kernel(x_ref, g_ref, w_ref, cos_ref, sin_ref, aq_ref, ak_ref, av_ref, bq_ref, bk_ref, bv_ref):
    y = _rms(x_ref[...], g_ref[...]).astype(jnp.bfloat16)
    q_scale = HEAD_DIM ** -0.5

    def proj(c0, width):
        return jnp.dot(y, w_ref[:, c0:c0 + width], preferred_element_type=jnp.float32)

    def roped(c0, width):
        return _rope(proj(c0, width), cos_ref[:, :width], sin_ref[:, :width])

    for c in range(A_Q // ROPE_CHUNK):
        aq_ref[:, c * ROPE_CHUNK:(c + 1) * ROPE_CHUNK] = (
            roped(c * ROPE_CHUNK, ROPE_CHUNK) * q_scale).astype(jnp.bfloat16)
    ak_ref[...] = roped(A_Q, A_KV).astype(jnp.bfloat16)
    av_ref[...] = proj(A_Q + A_KV, A_KV).astype(jnp.bfloat16)
    b0 = A_Q + 2 * A_KV
    for c in range(B_W // ROPE_CHUNK):
        sl = slice(c * ROPE_CHUNK, (c + 1) * ROPE_CHUNK)
        bq_ref[:, sl] = (roped(b0 + c * ROPE_CHUNK, ROPE_CHUNK) * q_scale).astype(jnp.bfloat16)
        bk_ref[:, sl] = roped(b0 + B_W + c * ROPE_CHUNK, ROPE_CHUNK).astype(jnp.bfloat16)
        bv_ref[:, sl] = proj(b0 + 2 * B_W + c * ROPE_CHUNK, ROPE_CHUNK).astype(jnp.bfloat16)


def _inproj(x2d, g, w_bf, cos, sin, seq):
    m = x2d.shape[0]
    tm = ROW_TILE
    pos_blocks = seq // tm
    row = lambda i: (i, 0)
    full = lambda i: (0, 0)
    widths = (A_Q, A_KV, A_KV, B_W, B_W, B_W)
    return pl.pallas_call(
        _inproj_kernel,
        grid=(m // tm,),
        in_specs=[
            pl.BlockSpec((tm, D_MODEL), row),
            pl.BlockSpec((1, D_MODEL), full),
            pl.BlockSpec(w_bf.shape, full),
            pl.BlockSpec((tm, ROPE_CHUNK), lambda i: (i % pos_blocks, 0)),
            pl.BlockSpec((tm, ROPE_CHUNK), lambda i: (i % pos_blocks, 0)),
        ],
        out_specs=[pl.BlockSpec((tm, w), row) for w in widths],
        out_shape=[jax.ShapeDtypeStruct((m, w), jnp.bfloat16) for w in widths],
        compiler_params=_params("parallel"),
        name="inproj",
    )(x2d, g, w_bf, cos, sin)


def _band_attn_kernel(*refs, tile, half, hq, hk, length, has_sink, has_stats):
    q_ref, kp_ref, kc_ref, kn_ref, vp_ref, vc_ref, vn_ref = refs[:7]
    rest = refs[7:]
    sink_ref = None
    if has_sink:
        sink_ref, rest = rest[0], rest[1:]
    o_ref = rest[0]
    st_ref = rest[1] if has_stats else None

    i = pl.program_id(2)
    span = tile + 2 * half
    k_all = jnp.concatenate([kp_ref[0, tile - half:, :], kc_ref[0], kn_ref[0, :half, :]], axis=0)
    v_all = jnp.concatenate([vp_ref[0, tile - half:, :], vc_ref[0], vn_ref[0, :half, :]], axis=0)
    qi = lax.broadcasted_iota(jnp.int32, (tile, span), 0)
    kj = lax.broadcasted_iota(jnp.int32, (tile, span), 1)
    kpos = i * tile - half + kj
    valid = (kpos >= 0) & (kpos < length) & (jnp.abs(kj - half - qi) <= half)

    group = hq // hk
    lane = lax.broadcasted_iota(jnp.int32, (tile, LANES), 1)
    stats = jnp.zeros((tile, LANES), jnp.float32)
    for kh in range(hk):
        k_h = k_all[:, kh * HEAD_DIM:(kh + 1) * HEAD_DIM]
        v_h = v_all[:, kh * HEAD_DIM:(kh + 1) * HEAD_DIM]
        for gi in range(group):
            h = kh * group + gi
            q_h = q_ref[0, :, h * HEAD_DIM:(h + 1) * HEAD_DIM]
            s = lax.dot_general(q_h, k_h, _NT, preferred_element_type=jnp.float32)
            s = jnp.where(valid, s, NEG)
            m = jnp.max(s, axis=-1, keepdims=True)
            if has_sink:
                m = jnp.maximum(m, sink_ref[h])
            p = jnp.exp(s - m)
            l = jnp.sum(p, axis=-1, keepdims=True)
            if has_sink:
                l = l + jnp.exp(sink_ref[h] - m)
            o = jnp.dot(p.astype(jnp.bfloat16), v_h, preferred_element_type=jnp.float32) / l
            o_ref[0, :, h * HEAD_DIM:(h + 1) * HEAD_DIM] = o.astype(o_ref.dtype)
            if has_stats:
                stats = jnp.where(lane == h, m + jnp.log(l), stats)
    if has_stats:
        st_ref[0] = stats


def _band_attn(q, k, v, *, dil, half, hq, hk, sink=None, stats=False):
    b, s, _ = q.shape
    length = s // dil
    tile = ATTN_TILE
    assert length % tile == 0 and half <= tile and hq <= LANES
    nb = length // tile
    cq, ck = hq * HEAD_DIM, hk * HEAD_DIM
    qv = q.reshape(b, length, dil * cq)
    kv = k.reshape(b, length, dil * ck)
    vv = v.reshape(b, length, dil * ck)
    cur = lambda bi, r, i: (bi, i, r)
    prev = lambda bi, r, i: (bi, jnp.maximum(i - 1, 0), r)
    nxt = lambda bi, r, i: (bi, jnp.minimum(i + 1, nb - 1), r)
    in_specs = [pl.BlockSpec((1, tile, cq), cur)]
    in_specs += [pl.BlockSpec((1, tile, ck), f) for f in (prev, cur, nxt)] * 2
    args = [qv, kv, kv, kv, vv, vv, vv]
    if sink is not None:
        in_specs.append(pl.BlockSpec(memory_space=pltpu.SMEM))
        args.append(sink)
    out_specs = [pl.BlockSpec((1, tile, cq), cur)]
    out_shape = [jax.ShapeDtypeStruct((b, length, dil * cq), jnp.bfloat16)]
    if stats:
        out_specs.append(pl.BlockSpec((1, tile, LANES), cur))
        out_shape.append(jax.ShapeDtypeStruct((b, length, dil * LANES), jnp.float32))
    outs = pl.pallas_call(
        functools.partial(_band_attn_kernel, tile=tile, half=half, hq=hq, hk=hk, length=length,
                          has_sink=sink is not None, has_stats=stats),
        grid=(b, dil, nb),
        in_specs=in_specs,
        out_specs=out_specs,
        out_shape=out_shape,
        compiler_params=_params("parallel", "parallel", "arbitrary"),
        name=f"band_attn_d{dil}_h{half}",
    )(*args)
    o = outs[0].reshape(b, s, cq)
    if stats:
        return o, outs[1].reshape(b, s, LANES)
    return o


def _outproj_kernel(a_ref, b1_ref, b2_ref, b3_ref, s1_ref, s2_ref, s3_ref, x_ref, ga_ref, gb_ref,
                    w_ref, e_ref, o_ref):
    s1, s2, s3 = s1_ref[...], s2_ref[...], s3_ref[...]
    mx = jnp.maximum(jnp.maximum(s1, s2), s3)
    e1, e2, e3 = jnp.exp(s1 - mx), jnp.exp(s2 - mx), jnp.exp(s3 - mx)
    den = e1 + e2 + e3
    expand = e_ref[...]

    def spread(alpha):
        return jnp.dot(alpha, expand, preferred_element_type=jnp.float32, precision=lax.Precision.HIGHEST)

    mix_b = (spread(e1 / den) * b1_ref[...].astype(jnp.float32)
             + spread(e2 / den) * b2_ref[...].astype(jnp.float32)
             + spread(e3 / den) * b3_ref[...].astype(jnp.float32))
    na = _rms(a_ref[...].astype(jnp.float32), ga_ref[...]).astype(jnp.bfloat16)
    nb = _rms(mix_b, gb_ref[...]).astype(jnp.bfloat16)
    out = jnp.dot(na, w_ref[:A_Q, :], preferred_element_type=jnp.float32)
    out = out + jnp.dot(nb, w_ref[A_Q:, :], preferred_element_type=jnp.float32)
    o_ref[...] = x_ref[...] + out


def _outproj(a, bs, sts, x2d, ga, gb, w_bf, expand):
    m = x2d.shape[0]
    tm = ROW_TILE
    row = lambda i: (i, 0)
    full = lambda i: (0, 0)
    return pl.pallas_call(
        _outproj_kernel,
        grid=(m // tm,),
        in_specs=[pl.BlockSpec((tm, A_Q), row)] + [pl.BlockSpec((tm, B_W), row)] * 3
        + [pl.BlockSpec((tm, LANES), row)] * 3
        + [pl.BlockSpec((tm, D_MODEL), row), pl.BlockSpec((1, A_Q), full), pl.BlockSpec((1, B_W), full),
           pl.BlockSpec(w_bf.shape, full), pl.BlockSpec(expand.shape, full)],
        out_specs=pl.BlockSpec((tm, D_MODEL), row),
        out_shape=jax.ShapeDtypeStruct((m, D_MODEL), jnp.float32),
        compiler_params=_params("parallel"),
        name="outproj",
    )(a, *bs, *sts, x2d, ga, gb, w_bf, expand)


def _memkv_kernel(m_ref, g_ref, w_ref, o_ref):
    y = _rms(m_ref[...], g_ref[...]).astype(jnp.bfloat16)
    o_ref[...] = jnp.dot(y, w_ref[...], preferred_element_type=jnp.float32).astype(jnp.bfloat16)


def _memkv(mem2d, g, w_bf):
    m = mem2d.shape[0]
    tm = ROW_TILE
    return pl.pallas_call(
        _memkv_kernel,
        grid=(m // tm,),
        in_specs=[pl.BlockSpec((tm, D_MODEL), lambda i: (i, 0)), pl.BlockSpec((1, D_MODEL), lambda i: (0, 0)),
                  pl.BlockSpec(w_bf.shape, lambda i: (0, 0))],
        out_specs=pl.BlockSpec((tm, 2 * X_W), lambda i: (i, 0)),
        out_shape=jax.ShapeDtypeStruct((m, 2 * X_W), jnp.bfloat16),
        compiler_params=_params("parallel"),
        name="memkv",
    )(mem2d, g, w_bf)


def _cross_kernel(x_ref, kv_ref, g_ref, wq_ref, wo_ref, o_ref):
    x = x_ref[0]
    xn = _rms(x, g_ref[...]).astype(jnp.bfloat16)
    xq = jnp.dot(xn, wq_ref[...], preferred_element_type=jnp.float32)
    heads = []
    for h in range(X_HEADS):
        sl = slice(h * X_HEAD_DIM, (h + 1) * X_HEAD_DIM)
        q_h = xq[:, sl].astype(jnp.bfloat16)
        k_h = kv_ref[0, :, h * X_HEAD_DIM:(h + 1) * X_HEAD_DIM]
        v_h = kv_ref[0, :, X_W + h * X_HEAD_DIM:X_W + (h + 1) * X_HEAD_DIM]
        s = lax.dot_general(q_h, k_h, _NT, preferred_element_type=jnp.float32) * (X_HEAD_DIM ** -0.5)
        p = jnp.exp(s - jnp.max(s, axis=-1, keepdims=True))
        l = jnp.sum(p, axis=-1, keepdims=True)
        heads.append(jnp.dot(p.astype(jnp.bfloat16), v_h, preferred_element_type=jnp.float32) / l)
    xo = jnp.concatenate(heads, axis=-1).astype(jnp.bfloat16)
    o_ref[0] = x + jnp.dot(xo, wo_ref[...], preferred_element_type=jnp.float32)


def _cross(x3d, kv3d, g, wq_bf, wo_bf):
    b, s, _ = x3d.shape
    tm = ROW_TILE
    mlen = kv3d.shape[1]
    return pl.pallas_call(
        _cross_kernel,
        grid=(b, s // tm),
        in_specs=[pl.BlockSpec((1, tm, D_MODEL), lambda bi, i: (bi, i, 0)),
                  pl.BlockSpec((1, mlen, 2 * X_W), lambda bi, i: (bi, 0, 0)),
                  pl.BlockSpec((1, D_MODEL), lambda bi, i: (0, 0)),
                  pl.BlockSpec(wq_bf.shape, lambda bi, i: (0, 0)),
                  pl.BlockSpec(wo_bf.shape, lambda bi, i: (0, 0))],
        out_specs=pl.BlockSpec((1, tm, D_MODEL), lambda bi, i: (bi, i, 0)),
        out_shape=jax.ShapeDtypeStruct(x3d.shape, jnp.float32),
        compiler_params=_params("parallel", "arbitrary"),
        name="cross",
    )(x3d, kv3d, g, wq_bf, wo_bf)


def _top16(vals, ids):
    top_v, top_i = [], []
    big = jnp.float32(1e9)
    for _ in range(PEER_TOPK):
        m = jnp.max(vals, axis=0, keepdims=True)
        sel = jnp.min(jnp.where(vals == m, ids, big), axis=0, keepdims=True)
        top_v.append(m)
        top_i.append(sel)
        vals = jnp.where(ids == sel, -jnp.inf, vals)
    return jnp.concatenate(top_v, axis=0), jnp.concatenate(top_i, axis=0)


def _pick_rows(table, idx):
    out = jnp.zeros_like(idx)
    for a in range(PEER_TOPK):
        out = jnp.where(idx == a, table[a:a + 1, :], out)
    return out


def _route_kernel(x_ref, g_ref, w_ref, keys_ref, hn_ref, eid_ref, gate_ref):
    tm = x_ref.shape[0]
    hn = _rms(x_ref[...], g_ref[...]).astype(jnp.bfloat16)
    hn_ref[...] = hn
    hq = jnp.dot(hn, w_ref[...], preferred_element_type=jnp.float32)

    key_ids = lax.broadcasted_iota(jnp.int32, (PEER_NKEYS, 1), 0).astype(jnp.float32)
    n_cand = PEER_TOPK + (PEER_TOPK - 1) * 8
    r = lax.broadcasted_iota(jnp.int32, (n_cand, 1), 0)
    rr = r - PEER_TOPK
    cand_ids = jnp.where(r < PEER_TOPK, r, (1 + (rr >> 3)) * PEER_TOPK + (rr & 7)).astype(jnp.float32)

    for h in range(PEER_HEADS):
        halves = []
        for p in range(2):
            c0 = (h * 2 + p) * PEER_NKEYS
            q_hp = hq[:, c0:c0 + PEER_NKEYS].astype(jnp.bfloat16)
            sc = lax.dot_general(keys_ref[h * 2 + p], q_hp, _NT, preferred_element_type=jnp.float32)
            halves.append(_top16(sc, key_ids))
        (s1, i1), (s2, i2) = halves
        chunks = [s1[0:1, :] + s2]
        for i in range(1, PEER_TOPK):
            chunks.append(s1[i:i + 1, :] + s2[:8, :])
        top_s, sel = _top16(jnp.concatenate(chunks, axis=0), cand_ids)
        sel_i = sel.astype(jnp.int32)
        ri = (sel_i >> 4).astype(jnp.float32)
        rj = (sel_i & 15).astype(jnp.float32)
        eid = _pick_rows(i1, ri) * PEER_NKEYS + _pick_rows(i2, rj)
        e = jnp.exp(top_s - top_s[0:1, :])
        gate = e / jnp.sum(e, axis=0, keepdims=True)
        eid_ref[h * PEER_TOPK:(h + 1) * PEER_TOPK, :] = eid.astype(jnp.int32)
        gate_ref[h * PEER_TOPK:(h + 1) * PEER_TOPK, :] = gate


def _route(x2d, g, wpq_bf, keys_bf):
    m = x2d.shape[0]
    tm = ROW_TILE
    return pl.pallas_call(
        _route_kernel,
        grid=(m // tm,),
        in_specs=[pl.BlockSpec((tm, D_MODEL), lambda i: (i, 0)), pl.BlockSpec((1, D_MODEL), lambda i: (0, 0)),
                  pl.BlockSpec(wpq_bf.shape, lambda i: (0, 0)), pl.BlockSpec(keys_bf.shape, lambda i: (0, 0, 0))],
        out_specs=[pl.BlockSpec((tm, D_MODEL), lambda i: (i, 0)),
                   pl.BlockSpec((PEER_PAIRS, tm), lambda i: (0, i)),
                   pl.BlockSpec((PEER_PAIRS, tm), lambda i: (0, i))],
        out_shape=[jax.ShapeDtypeStruct((m, D_MODEL), jnp.bfloat16),
                   jax.ShapeDtypeStruct((PEER_PAIRS, m), jnp.int32),
                   jax.ShapeDtypeStruct((PEER_PAIRS, m), jnp.float32)],
        compiler_params=_params("parallel"),
        name="route",
    )(x2d, g, wpq_bf, keys_bf)


def _bf16_bits(x):
    return pltpu.bitcast(x.astype(jnp.bfloat16).astype(jnp.float32), jnp.uint32)


def _pack_pair(lo, hi):
    return (_bf16_bits(lo) >> 16) | (_bf16_bits(hi) & jnp.uint32(0xFFFF0000))


def _pack_kernel(u_ref, v_ref, o_ref):
    o_ref[:, :HALF_D] = _pack_pair(u_ref[:, :HALF_D], u_ref[:, HALF_D:])
    o_ref[:, HALF_D:] = _pack_pair(v_ref[:, :HALF_D], v_ref[:, HALF_D:])


def _pack_tables(u_tab, v_tab):
    n = u_tab.shape[0]
    tr = ROW_TILE
    return pl.pallas_call(
        _pack_kernel,
        grid=(n // tr,),
        in_specs=[pl.BlockSpec((tr, D_MODEL), lambda i: (i, 0))] * 2,
        out_specs=pl.BlockSpec((tr, D_MODEL), lambda i: (i, 0)),
        out_shape=jax.ShapeDtypeStruct((n, D_MODEL), jnp.uint32),
        compiler_params=_params("parallel"),
        name="pack_tables",
    )(u_tab, v_tab)


def _expert_kernel(eid_ref, eid_next_ref, gate_ref, hn_ref, x_ref, g_ref, tab_ref, o_ref, buf, sem):
    tt = EXPERT_TILE
    rows = tt * PEER_PAIRS
    i = pl.program_id(0)
    n = pl.num_programs(0)
    slot = i % 2

    def row_copy(ids_ref, j, s):
        return pltpu.make_async_copy(tab_ref.at[pl.ds(ids_ref[0, 0, j], 1)], buf.at[s, pl.ds(j, 1)], sem.at[s])

    def gather(ids_ref, s):
        def body(j, carry):
            row_copy(ids_ref, j, s).start()
            return carry
        lax.fori_loop(0, rows, body, 0, unroll=8)

    @pl.when(i == 0)
    def _():
        gather(eid_ref, 0)

    @pl.when(i + 1 < n)
    def _():
        gather(eid_next_ref, 1 - slot)

    pltpu.make_async_copy(tab_ref.at[pl.ds(0, rows)], buf.at[slot], sem.at[slot]).wait()

    hn = hn_ref[...]
    h_lo, h_hi = hn[:, :HALF_D], hn[:, HALF_D:]
    n_virt = 2 * PEER_PAIRS
    lane = lax.broadcasted_iota(jnp.int32, (tt, n_virt), 1)
    rowi = lax.broadcasted_iota(jnp.int32, (tt, n_virt), 0)
    even = (lane & 1) == 0

    pre = jnp.zeros((tt, n_virt), jnp.float32)
    for t in range(tt):
        words = buf[slot, t * PEER_PAIRS:(t + 1) * PEER_PAIRS, :HALF_D]
        bu = pltpu.bitcast(words, jnp.bfloat16)
        r_lo = lax.dot_general(h_lo, bu, _NT, preferred_element_type=jnp.float32)
        r_hi = lax.dot_general(h_hi, bu, _NT, preferred_element_type=jnp.float32)
        pre = jnp.where(rowi == t, jnp.where(even, r_lo, r_hi), pre)
    pre = pre + jnp.where(even, pltpu.roll(pre, n_virt - 1, 1), pltpu.roll(pre, 1, 1))
    act = 0.5 * pre * (1.0 + lax.erf(pre * (2.0 ** -0.5)))
    wgt = gate_ref[...] * act

    out_lo = jnp.zeros((tt, HALF_D), jnp.float32)
    out_hi = jnp.zeros((tt, HALF_D), jnp.float32)
    for t in range(tt):
        words = buf[slot, t * PEER_PAIRS:(t + 1) * PEER_PAIRS, HALF_D:]
        bv = pltpu.bitcast(words, jnp.bfloat16)
        w_t = jnp.where(rowi == t, wgt, 0.0)
        out_lo = out_lo + jnp.dot(jnp.where(even, w_t, 0.0).astype(jnp.bfloat16), bv,
                                  preferred_element_type=jnp.float32)
        out_hi = out_hi + jnp.dot(jnp.where(even, 0.0, w_t).astype(jnp.bfloat16), bv,
                                  preferred_element_type=jnp.float32)
    x3 = x_ref[...] + jnp.concatenate([out_lo, out_hi], axis=-1)
    o_ref[...] = _rms(x3, g_ref[...])


def _experts(eid_tiles, gate2, hn, x2d, g_final, table):
    m = x2d.shape[0]
    tt = EXPERT_TILE
    n = m // tt
    rows = tt * PEER_PAIRS
    smem_cur = pl.BlockSpec((1, 1, rows), lambda i: (i, 0, 0), memory_space=pltpu.SMEM)
    smem_next = pl.BlockSpec((1, 1, rows), lambda i: (jnp.minimum(i + 1, n - 1), 0, 0), memory_space=pltpu.SMEM)
    return pl.pallas_call(
        _expert_kernel,
        grid=(n,),
        in_specs=[smem_cur, smem_next,
                  pl.BlockSpec((tt, 2 * PEER_PAIRS), lambda i: (i, 0)),
                  pl.BlockSpec((tt, D_MODEL), lambda i: (i, 0)),
                  pl.BlockSpec((tt, D_MODEL), lambda i: (i, 0)),
                  pl.BlockSpec((1, D_MODEL), lambda i: (0, 0)),
                  pl.BlockSpec(memory_space=pl.ANY)],
        out_specs=pl.BlockSpec((tt, D_MODEL), lambda i: (i, 0)),
        out_shape=jax.ShapeDtypeStruct((m, D_MODEL), jnp.float32),
        scratch_shapes=[pltpu.VMEM((2, rows, D_MODEL), jnp.uint32), pltpu.SemaphoreType.DMA((2,))],
        compiler_params=_params("arbitrary"),
        name="experts",
    )(eid_tiles, eid_tiles, gate2, hn, x2d, g_final, table)


def _rope_tables(seq):
    pos = jnp.arange(seq, dtype=jnp.float32)
    freqs = jnp.power(ROPE_THETA, -jnp.arange(0, HEAD_DIM, 2, dtype=jnp.float32) / HEAD_DIM)
    ang = pos[:, None] * freqs[None, :]
    cos, sin = jnp.cos(ang), jnp.sin(ang)
    reps = ROPE_CHUNK // HEAD_DIM
    return (jnp.tile(jnp.concatenate([cos, cos], axis=-1), (1, reps)),
            jnp.tile(jnp.concatenate([-sin, sin], axis=-1), (1, reps)))


def _trunk(x, mem, w, table):
    b, s, _ = x.shape
    m = b * s
    x2d = x.reshape(m, D_MODEL)
    cos, sin = _rope_tables(s)
    aq, ak, av, bq, bk, bv = _inproj(x2d, w["g_mix"], w["w_in"], cos, sin, s)
    a_out = _band_attn(aq.reshape(b, s, A_Q), ak.reshape(b, s, A_KV), av.reshape(b, s, A_KV),
                       dil=1, half=A_WINDOW, hq=A_HEADS, hk=A_KV_HEADS, sink=w["a_sink"])
    bq3, bk3, bv3 = (t.reshape(b, s, B_W) for t in (bq, bk, bv))
    b_outs, b_stats = [], []
    for win, dil in B_PATTERNS:
        o, st = _band_attn(bq3, bk3, bv3, dil=dil, half=(win // 2) // dil, hq=B_HEADS, hk=B_HEADS, stats=True)
        b_outs.append(o.reshape(m, B_W))
        b_stats.append(st.reshape(m, LANES))
    x1 = _outproj(a_out.reshape(m, A_Q), b_outs, b_stats, x2d, w["g_a_out"], w["g_b_out"], w["w_out"],
                  w["expand"])
    kv = _memkv(mem.reshape(-1, D_MODEL), w["g_mem"], w["w_xkv"]).reshape(b, mem.shape[1], 2 * X_W)
    x2 = _cross(x1.reshape(b, s, D_MODEL), kv, w["g_cross"], w["w_xq"], w["w_xo"]).reshape(m, D_MODEL)
    hn, eid_t, gate_t = _route(x2, w["g_ffn"], w["w_pq"], w["peer_keys"])
    eid_tiles = eid_t.T.reshape(m // EXPERT_TILE, 1, EXPERT_TILE * PEER_PAIRS)
    gate2 = jnp.repeat(gate_t.T, 2, axis=1)
    y = _experts(eid_tiles, gate2, hn, x2, w["g_final"], table)
    return y.reshape(b, s, D_MODEL)


def kernel(x_prompt, x_sample, mem_prompt, mem_sample, g_mix, w_in, a_sink, g_a_out, g_b_out, w_out, g_cross,
           g_mem, w_xq, w_xkv, w_xo, g_ffn, w_pq, peer_keys, peer_u, peer_v, g_final):
    bf = jnp.bfloat16
    head = lax.broadcasted_iota(jnp.int32, (LANES, B_W), 0)
    col = lax.broadcasted_iota(jnp.int32, (LANES, B_W), 1)
    w = {
        "g_mix": g_mix[0][None, :], "w_in": w_in[0].astype(bf), "a_sink": a_sink[0],
        "g_a_out": g_a_out[0][None, :], "g_b_out": g_b_out[0][None, :], "w_out": w_out[0].astype(bf),
        "expand": (col // HEAD_DIM == head).astype(jnp.float32),
        "g_cross": g_cross[0][None, :], "g_mem": g_mem[0][None, :],
        "w_xq": w_xq[0].astype(bf), "w_xkv": w_xkv[0].astype(bf), "w_xo": w_xo[0].astype(bf),
        "g_ffn": g_ffn[0][None, :], "w_pq": w_pq[0].astype(bf),
        "peer_keys": peer_keys[0].reshape(PEER_HEADS * 2, PEER_NKEYS, -1).astype(bf),
        "g_final": g_final[None, :],
    }
    table = _pack_tables(peer_u[0], peer_v[0])
    return _trunk(x_prompt, mem_prompt, w, table), _trunk(x_sample, mem_sample, w, table)
```
